```python
import math
import jax, jax.numpy as jnp
from jax import lax
import numpy as np

D_MODEL = 2048
BATCH = 4
SEQ = 2048
DEPTH = 2
DEC_BATCH = 128
DEC_SEQ = 4
PAST_LEN = 16384
PAGE_SIZE = 128

D_MIX = 2 * D_MODEL
W_A = D_MIX // 2
S5_CH = 16
S5_GROUPS = W_A // S5_CH
S5_P = 64
W_S = D_MIX - W_A
SSD_HEADDIM = 64
SSD_HEADS = W_S // SSD_HEADDIM
SSD_GROUPS = 8
SSD_N = 128
CONV_K = 4
CONV_DIM = W_S + 2 * SSD_GROUPS * SSD_N
IN_COLS = 2 * W_A + CONV_DIM + W_S + SSD_HEADS
SSD_CHUNK = 128
EPS = 1e-5
DT_MIN = 1e-3
DT_MAX = 1e-1
LAMBDA_RE_MAX = -1e-4

kernel_name = "hymba_s5_ssd_hybrid_step"


def rmsnorm(x, w):
    xf = x.astype(jnp.float32)
    y = xf * lax.rsqrt(jnp.mean(xf * xf, axis=-1, keepdims=True) + EPS)
    return (y * w.astype(jnp.float32)).astype(x.dtype)


def s5_branch(u, h0_re, h0_im, lam_re, lam_im, log_step, b_re, b_im, c_re, c_im, d):
    f32 = jnp.float32
    bsz, L, _ = u.shape
    lr = jnp.minimum(lam_re.astype(f32), LAMBDA_RE_MAX)
    li = lam_im.astype(f32)
    step = jnp.exp(log_step.astype(f32))[:, None]
    mag = jnp.exp(lr * step)
    ab_re = mag * jnp.cos(li * step)
    ab_im = mag * jnp.sin(li * step)
    den = lr * lr + li * li
    nr = ab_re - 1.0
    g_re = (nr * lr + ab_im * li) / den
    g_im = (ab_im * lr - nr * li) / den
    br = b_re.astype(f32)
    bi = b_im.astype(f32)
    bb_re = g_re[..., None] * br - g_im[..., None] * bi
    bb_im = g_re[..., None] * bi + g_im[..., None] * br
    uf = u.astype(f32)
    ug = uf.reshape(bsz, L, S5_GROUPS, S5_CH)
    bu_re = jnp.einsum("blgc,gpc->lbgp", ug, bb_re)
    bu_im = jnp.einsum("blgc,gpc->lbgp", ug, bb_im)
    h0r = h0_re.astype(f32)
    h0i = h0_im.astype(f32)
    bu_re = bu_re.at[0].add(ab_re * h0r - ab_im * h0i)
    bu_im = bu_im.at[0].add(ab_re * h0i + ab_im * h0r)
    a_re = jnp.broadcast_to(ab_re, (L, 1, S5_GROUPS, S5_P))
    a_im = jnp.broadcast_to(ab_im, (L, 1, S5_GROUPS, S5_P))

    def combine(e1, e2):
        a1r, a1i, b1r, b1i = e1
        a2r, a2i, b2r, b2i = e2
        return (a2r * a1r - a2i * a1i,
                a2r * a1i + a2i * a1r,
                a2r * b1r - a2i * b1i + b2r,
                a2r * b1i + a2i * b1r + b2i)

    _, _, hr, hi = lax.associative_scan(combine, (a_re, a_im, bu_re, bu_im), axis=0)
    y = (jnp.einsum("lbgp,gcp->blgc", hr, c_re.astype(f32))
         - jnp.einsum("lbgp,gcp->blgc", hi, c_im.astype(f32)))
    y = y.reshape(bsz, L, W_A) + d.astype(f32) * uf
    return y.astype(u.dtype), hr[-1], hi[-1]


def causal_conv(xbc, buf, w, b):
    L = xbc.shape[1]
    xp = jnp.concatenate([buf.astype(xbc.dtype), xbc], axis=1)
    out = b
    for k in range(CONV_K):
        out = out + xp[:, k:k + L] * w[k]
    return jax.nn.silu(out), xp[:, L:]


def segsum(a):
    T = a.shape[-1]
    cs = jnp.cumsum(a, axis=-1)
    diff = cs[..., :, None] - cs[..., None, :]
    mask = jnp.tril(jnp.ones((T, T), dtype=bool))
    return jnp.where(mask, diff, -jnp.inf)


def ssd_scan(x, dt, a, bm, cm, h0):
    f32 = jnp.float32
    bsz, L = x.shape[:2]
    T = math.gcd(L, SSD_CHUNK)
    nc = L // T
    R = SSD_HEADS // SSD_GROUPS
    xd = (x.astype(f32) * dt[..., None]).reshape(bsz, nc, T, SSD_GROUPS, R, SSD_HEADDIM)
    da = (dt * a).reshape(bsz, nc, T, SSD_GROUPS, R).transpose(0, 3, 4, 1, 2)
    bc = bm.astype(f32).reshape(bsz, nc, T, SSD_GROUPS, SSD_N)
    cc = cm.astype(f32).reshape(bsz, nc, T, SSD_GROUPS, SSD_N)
    da_cs = jnp.cumsum(da, axis=-1)
    decay = jnp.exp(segsum(da))
    cb = jnp.einsum("bctgn,bcsgn->bgcts", cc, bc)
    y_diag = jnp.einsum("bgcts,bgrcts,bcsgrp->bctgrp", cb, decay, xd)
    decay_states = jnp.exp(da_cs[..., -1:] - da_cs)
    states = jnp.einsum("bctgn,bgrct,bctgrp->bcgrpn", bc, decay_states, xd)
    h0r = h0.astype(f32).reshape(bsz, 1, SSD_GROUPS, R, SSD_HEADDIM, SSD_N)
    states = jnp.concatenate([h0r, states], axis=1)
    chunk_ends = jnp.pad(da_cs[..., -1], ((0, 0), (0, 0), (0, 0), (1, 0)))
    chunk_decay = jnp.exp(segsum(chunk_ends))
    states = jnp.einsum("bgrzc,bcgrpn->bzgrpn", chunk_decay, states)
    prev_states, final = states[:, :-1], states[:, -1]
    y_off = jnp.einsum("bctgn,bcgrpn,bgrct->bctgrp", cc, prev_states, jnp.exp(da_cs))
    y = (y_diag + y_off).reshape(bsz, L, SSD_HEADS, SSD_HEADDIM)
    return y, final.reshape(bsz, SSD_HEADS, SSD_HEADDIM, SSD_N)


def mixer_layer(x, s5_h_re, s5_h_im, ssd_h, conv_buf,
                norm_w, w_in, lam_re, lam_im, log_step, b_re, b_im, c_re, c_im, s5_d,
                glu_w, glu_b, s5_norm_w, conv_w, conv_b, dt_bias, a_log, ssd_d, ssd_norm_w, w_out):
    bsz, L, _ = x.shape
    f32 = jnp.float32
    h = rmsnorm(x, norm_w)
    proj = h @ w_in
    u_a, z_a, xbc, z_s, dt_raw = jnp.split(
        proj, [W_A, 2 * W_A, 2 * W_A + CONV_DIM, 2 * W_A + CONV_DIM + W_S], axis=-1)

    y_a, s5_re, s5_im = s5_branch(u_a, s5_h_re, s5_h_im, lam_re, lam_im, log_step,
                                  b_re, b_im, c_re, c_im, s5_d)
    g = jax.nn.gelu(y_a)
    y_a = g * jax.nn.sigmoid(g @ glu_w + glu_b)
    y_a = rmsnorm(y_a * jax.nn.silu(z_a), s5_norm_w).astype(x.dtype)

    xbc, conv_new = causal_conv(xbc, conv_buf, conv_w, conv_b)
    xs, bm, cm = jnp.split(xbc, [W_S, W_S + SSD_GROUPS * SSD_N], axis=-1)
    dt = jax.nn.softplus(dt_raw.astype(f32) + dt_bias.astype(f32))
    a = -jnp.exp(a_log.astype(f32))
    xs_h = xs.reshape(bsz, L, SSD_HEADS, SSD_HEADDIM)
    y_s, ssd_new = ssd_scan(xs_h, dt, a,
                            bm.reshape(bsz, L, SSD_GROUPS, SSD_N),
                            cm.reshape(bsz, L, SSD_GROUPS, SSD_N), ssd_h)
    y_s = y_s + ssd_d.astype(f32)[:, None] * xs_h.astype(f32)
    y_s = y_s.reshape(bsz, L, W_S) * jax.nn.silu(z_s.astype(f32))
    y_s = rmsnorm(y_s, ssd_norm_w).astype(x.dtype)

    out = jnp.concatenate([y_a, y_s], axis=-1) @ w_out
    sd = s5_h_re.dtype
    return (x + out, s5_re.astype(sd), s5_im.astype(sd),
            ssd_new.astype(ssd_h.dtype), conv_new.astype(conv_buf.dtype))


def setup_inputs(seed: int = 0) -> dict:
    key = jax.random.key(seed)
    ks = iter(jax.random.split(key, 40))
    f32 = jnp.float32

    def nrm(shape, s):
        return jax.random.normal(next(ks), shape, f32) * s

    x_prompt = nrm((BATCH, SEQ, D_MODEL), 1.0)
    x_sample = nrm((DEC_BATCH, DEC_SEQ, D_MODEL), 1.0)
    state_s5_re = nrm((DEPTH, DEC_BATCH, S5_GROUPS, S5_P), 0.3)
    state_s5_im = nrm((DEPTH, DEC_BATCH, S5_GROUPS, S5_P), 0.3)
    state_ssd = nrm((DEPTH, DEC_BATCH, SSD_HEADS, SSD_HEADDIM, SSD_N), 0.3)
    cache_conv = nrm((DEPTH, DEC_BATCH, CONV_K - 1, CONV_DIM), 1.0)

    norm_w = 1.0 + nrm((DEPTH, D_MODEL), 0.02)
    w_in = nrm((DEPTH, D_MODEL, IN_COLS), D_MODEL ** -0.5)
    n_idx = jnp.arange(S5_P, dtype=f32)
    s5_lambda_re = -0.5 + nrm((DEPTH, S5_GROUPS, S5_P), 0.01)
    s5_lambda_im = math.pi * n_idx + nrm((DEPTH, S5_GROUPS, S5_P), 0.01)
    s5_log_step = jax.random.uniform(next(ks), (DEPTH, S5_GROUPS), f32,
                                     math.log(DT_MIN), math.log(DT_MAX))
    s5_b_re = nrm((DEPTH, S5_GROUPS, S5_P, S5_CH), (2 * S5_CH) ** -0.5)
    s5_b_im = nrm((DEPTH, S5_GROUPS, S5_P, S5_CH), (2 * S5_CH) ** -0.5)
    s5_c_re = nrm((DEPTH, S5_GROUPS, S5_CH, S5_P), S5_P ** -0.5)
    s5_c_im = nrm((DEPTH, S5_GROUPS, S5_CH, S5_P), S5_P ** -0.5)
    s5_d = nrm((DEPTH, W_A), 1.0)
    s5_glu_w = nrm((DEPTH, W_A, W_A), W_A ** -0.5)
    s5_glu_b = nrm((DEPTH, W_A), 0.01)
    s5_norm_w = 1.0 + nrm((DEPTH, W_A), 0.02)
    conv_w = nrm((DEPTH, CONV_K, CONV_DIM), CONV_K ** -0.5)
    conv_b = nrm((DEPTH, CONV_DIM), 0.01)
    dt0 = jnp.exp(jax.random.uniform(next(ks), (DEPTH, SSD_HEADS), f32,
                                     math.log(DT_MIN), math.log(DT_MAX)))
    dt_bias = dt0 + jnp.log(-jnp.expm1(-dt0))
    a_log = jnp.log(jax.random.uniform(next(ks), (DEPTH, SSD_HEADS), f32, 1.0, 16.0))
    ssd_d = 1.0 + nrm((DEPTH, SSD_HEADS), 0.02)
    ssd_norm_w = 1.0 + nrm((DEPTH, W_S), 0.02)
    w_out = nrm((DEPTH, D_MIX, D_MODEL), D_MIX ** -0.5)
    final_norm_w = 1.0 + nrm((D_MODEL,), 0.02)
    return {
        "x_prompt": x_prompt, "x_sample": x_sample,
        "state_s5_re": state_s5_re, "state_s5_im": state_s5_im,
        "state_ssd": state_ssd, "cache_conv": cache_conv,
        "norm_w": norm_w, "w_in": w_in,
        "s5_lambda_re": s5_lambda_re, "s5_lambda_im": s5_lambda_im,
        "s5_log_step": s5_log_step, "s5_b_re": s5_b_re, "s5_b_im": s5_b_im,
        "s5_c_re": s5_c_re, "s5_c_im": s5_c_im, "s5_d": s5_d,
        "s5_glu_w": s5_glu_w, "s5_glu_b": s5_glu_b, "s5_norm_w": s5_norm_w,
        "conv_w": conv_w, "conv_b": conv_b, "dt_bias": dt_bias, "a_log": a_log,
        "ssd_d": ssd_d, "ssd_norm_w": ssd_norm_w, "w_out": w_out,
        "final_norm_w": final_norm_w,
    }


def reference(x_prompt, x_sample, state_s5_re, state_s5_im, state_ssd, cache_conv,
              norm_w, w_in, s5_lambda_re, s5_lambda_im, s5_log_step, s5_b_re, s5_b_im,
              s5_c_re, s5_c_im, s5_d, s5_glu_w, s5_glu_b, s5_norm_w,
              conv_w, conv_b, dt_bias, a_log, ssd_d, ssd_norm_w, w_out, final_norm_w):
    bp = x_prompt.shape[0]
    dtp = x_prompt.dtype
    z_s5 = jnp.zeros((bp, S5_GROUPS, S5_P), dtp)
    z_ssd = jnp.zeros((bp, SSD_HEADS, SSD_HEADDIM, SSD_N), dtp)
    z_conv = jnp.zeros((bp, CONV_K - 1, CONV_DIM), dtp)
    hp, hs = x_prompt, x_sample
    p_re, p_im, p_ssd, p_conv = [], [], [], []
    s_re, s_im, s_ssd, s_conv = [], [], [], []
    for l in range(DEPTH):
        lw = (norm_w[l], w_in[l], s5_lambda_re[l], s5_lambda_im[l], s5_log_step[l],
              s5_b_re[l], s5_b_im[l], s5_c_re[l], s5_c_im[l], s5_d[l],
              s5_glu_w[l], s5_glu_b[l], s5_norm_w[l], conv_w[l], conv_b[l],
              dt_bias[l], a_log[l], ssd_d[l], ssd_norm_w[l], w_out[l])
        hp, a1, a2, a3, a4 = mixer_layer(hp, z_s5, z_s5, z_ssd, z_conv, *lw)
        hs, b1, b2, b3, b4 = mixer_layer(hs, state_s5_re[l], state_s5_im[l],
                                         state_ssd[l], cache_conv[l], *lw)
        p_re.append(a1); p_im.append(a2); p_ssd.append(a3); p_conv.append(a4)
        s_re.append(b1); s_im.append(b2); s_ssd.append(b3); s_conv.append(b4)
    y_prompt = rmsnorm(hp, final_norm_w)
    y_sample = rmsnorm(hs, final_norm_w)
    return (y_prompt, y_sample,
            jnp.stack(p_re), jnp.stack(p_im), jnp.stack(p_ssd), jnp.stack(p_conv),
            jnp.stack(s_re), jnp.stack(s_im), jnp.stack(s_ssd), jnp.stack(s_conv))
```

```python
import functools
import math

import numpy as np
import jax
import jax.numpy as jnp
from jax import lax
from jax.experimental import pallas as pl
from jax.experimental.pallas import tpu as pltpu

D_MODEL = 2048
W_A = 2048
S5_CH = 16
S5_GROUPS = 128
S5_P = 64
S5_STATE = S5_GROUPS * S5_P
W_S = 2048
SSD_HEADDIM = 64
SSD_HEADS = 32
SSD_GROUPS = 8
SSD_N = 128
CONV_K = 4
CONV_DIM = W_S + 2 * SSD_GROUPS * SSD_N
MAIN_COLS = 2 * W_A + CONV_DIM + W_S
EPS = 1e-5
LAMBDA_RE_MAX = -1e-4

LANES = 128
SUBLANES = 8
ROWS = 128
S5_QBLOCKS = 8
S5_QCH = W_A // S5_QBLOCKS
S5_QST = S5_STATE // S5_QBLOCKS
SCAN_LANES = 512
VMEM_LIMIT_BYTES = 56 * 1024 * 1024

F32 = jnp.float32
BF16 = jnp.bfloat16


def _tile(m, pref):
    t = pref
    while m % t:
        t //= 2
    assert t >= SUBLANES, (m, pref)
    return t


def _params(sem):
    return pltpu.CompilerParams(dimension_semantics=sem, vmem_limit_bytes=VMEM_LIMIT_BYTES)


def _split2(x):
    hi = x.astype(BF16)
    lo = (x - hi.astype(F32)).astype(BF16)
    return hi, lo


def _split3(x):
    hi = x.astype(BF16)
    r = x - hi.astype(F32)
    mid = r.astype(BF16)
    lo = (r - mid.astype(F32)).astype(BF16)
    return hi, mid, lo


def _dot(a, b):
    return jnp.dot(a, b, preferred_element_type=F32)


def _dot_nt(a, b):
    return lax.dot_general(a, b, (((1,), (1,)), ((), ())), preferred_element_type=F32)


def _dot_parts(parts, b):
    acc = _dot(parts[0], b)
    for p in parts[1:]:
        acc = acc + _dot(p, b)
    return acc


def _rmsnorm_kernel(x_ref, w_ref, o_ref):
    x = x_ref[...]
    y = x * lax.rsqrt(jnp.mean(x * x, axis=-1, keepdims=True) + EPS)
    o_ref[...] = (y * w_ref[...]).astype(o_ref.dtype)


def _rmsnorm_bf16(x, w):
    m, d = x.shape
    tm = _tile(m, 512)
    return pl.pallas_call(
        _rmsnorm_kernel,
        grid=(m // tm,),
        in_specs=[pl.BlockSpec((tm, d), lambda i: (i, 0)), pl.BlockSpec((1, d), lambda i: (0, 0))],
        out_specs=pl.BlockSpec((tm, d), lambda i: (i, 0)),
        out_shape=jax.ShapeDtypeStruct((m, d), BF16),
        compiler_params=_params(("parallel",)),
        name="rmsnorm_bf16",
    )(x, w.reshape(1, d))


def _matmul_kernel(*refs, has_res, has_norm):
    a_ref, b_ref = refs[0], refs[1]
    k = 2
    acc = _dot(a_ref[...], b_ref[...])
    if has_res:
        acc = acc + refs[k][...]
        k += 1
    if has_norm:
        acc = acc * lax.rsqrt(jnp.mean(acc * acc, axis=-1, keepdims=True) + EPS) * refs[k][...]
        k += 1
    refs[k][...] = acc


def _matmul(a, b, *, tm, tn, res=None, norm_w=None, name):
    m, kk = a.shape
    n = b.shape[1]
    tm = _tile(m, tm)
    assert n % tn == 0 and (norm_w is None or tn == n)
    in_specs = [pl.BlockSpec((tm, kk), lambda j, i: (i, 0)), pl.BlockSpec((kk, tn), lambda j, i: (0, j))]
    args = [a, b]
    if res is not None:
        in_specs.append(pl.BlockSpec((tm, tn), lambda j, i: (i, j)))
        args.append(res)
    if norm_w is not None:
        in_specs.append(pl.BlockSpec((1, tn), lambda j, i: (0, j)))
        args.append(norm_w.reshape(1, n))
    return pl.pallas_call(
        functools.partial(_matmul_kernel, has_res=res is not None, has_norm=norm_w is not None),
        grid=(n // tn, m // tm),
        in_specs=in_specs,
        out_specs=pl.BlockSpec((tm, tn), lambda j, i: (i, j)),
        out_shape=jax.ShapeDtypeStruct((m, n), F32),
        compiler_params=_params(("parallel", "parallel")),
        name=name,
    )(*args)


def _s5_prep_kernel(lr_ref, li_ref, ls_ref, br_ref, bi_ref, lr2_ref, li2_ref, ls2_ref,
                    bbr_ref, bbi_ref, abr_ref, abi_ref):
    def zoh(lr, li, ls):
        lr = jnp.minimum(lr, LAMBDA_RE_MAX)
        step = jnp.exp(ls)
        mag = jnp.exp(lr * step)
        ab_re = mag * jnp.cos(li * step)
        ab_im = mag * jnp.sin(li * step)
        den = lr * lr + li * li
        nr = ab_re - 1.0
        g_re = (nr * lr + ab_im * li) / den
        g_im = (ab_im * lr - nr * li) / den
        return ab_re, ab_im, g_re, g_im

    _, _, g_re, g_im = zoh(lr_ref[...], li_ref[...], ls_ref[...])
    br = br_ref[...]
    bi = bi_ref[...]
    bbr_ref[...] = g_re * br - g_im * bi
    bbi_ref[...] = g_re * bi + g_im * br
    ab_re, ab_im, _, _ = zoh(lr2_ref[...], li2_ref[...], ls2_ref[...])
    abr_ref[...] = ab_re
    abi_ref[...] = ab_im


def _s5_prep(lam_re, lam_im, log_step, b_re, b_im):
    g, p = lam_re.shape
    rep = lambda x: jnp.repeat(x, S5_CH, axis=-1)
    ls = jnp.broadcast_to(log_step[:, None], (g, p))
    pc = p * S5_CH
    out = pl.pallas_call(
        _s5_prep_kernel,
        out_shape=[jax.ShapeDtypeStruct((g, pc), F32), jax.ShapeDtypeStruct((g, pc), F32),
                   jax.ShapeDtypeStruct((g, p), F32), jax.ShapeDtypeStruct((g, p), F32)],
        name="s5_prep",
    )(rep(lam_re), rep(lam_im), rep(ls), b_re.reshape(g, pc), b_im.reshape(g, pc), lam_re, lam_im, ls)
    bb_re, bb_im, ab_re, ab_im = out
    return bb_re.reshape(g, p, S5_CH), bb_im.reshape(g, p, S5_CH), ab_re, ab_im


def _blockdiag_in(bb):
    gq = S5_GROUPS // S5_QBLOCKS
    t = bb.reshape(S5_QBLOCKS, gq, S5_P, S5_CH).transpose(0, 1, 3, 2)
    eye = jnp.eye(gq, dtype=bb.dtype)
    return jnp.einsum("qgcp,gh->qgchp", t, eye).reshape(S5_QBLOCKS, S5_QCH, S5_QST).astype(BF16)


def _blockdiag_out(c):
    gq = S5_GROUPS // S5_QBLOCKS
    t = c.reshape(S5_QBLOCKS, gq, S5_CH, S5_P)
    eye = jnp.eye(gq, dtype=c.dtype)
    return jnp.einsum("qgcp,hg->qhpgc", t, eye).reshape(S5_QBLOCKS, S5_QST, S5_QCH).astype(BF16)


def _s5_kernel(*refs, steps, rps, perm, carry):
    it = iter(refs)
    u_ref = next(it)
    bre_ref, bim_ref, cre_ref, cim_ref = next(it), next(it), next(it), next(it)
    are_ref, aim_ref, d_ref = next(it), next(it), next(it)
    p_ref = pt_ref = h0r_ref = h0i_ref = None
    if perm:
        p_ref, pt_ref = next(it), next(it)
    if not carry:
        h0r_ref, h0i_ref = next(it), next(it)
    y_ref, hr_out, hi_out = next(it), next(it), next(it)
    sre, sim, ysc = next(it), next(it), next(it)
    cre_s = cim_s = None
    if carry:
        cre_s, cim_s = next(it), next(it)

    u = u_ref[...]
    ub = u.astype(BF16)
    if perm:
        ub = _dot(p_ref[...], ub).astype(BF16)
    for q in range(S5_QBLOCKS):
        uq = ub[:, q * S5_QCH:(q + 1) * S5_QCH]
        sre[:, q * S5_QST:(q + 1) * S5_QST] = _dot(uq, bre_ref[q])
        sim[:, q * S5_QST:(q + 1) * S5_QST] = _dot(uq, bim_ref[q])

    if carry:
        @pl.when(pl.program_id(1) == 0)
        def _():
            cre_s[...] = jnp.zeros_like(cre_s)
            cim_s[...] = jnp.zeros_like(cim_s)

    for cb in range(S5_STATE // SCAN_LANES):
        ln = slice(cb * SCAN_LANES, (cb + 1) * SCAN_LANES)
        a_re = are_ref[:, ln]
        a_im = aim_ref[:, ln]
        if carry:
            h0 = (cre_s[0:rps, ln], cim_s[0:rps, ln])
        else:
            h0 = (h0r_ref[:, ln], h0i_ref[:, ln])

        def step(t, h, ln=ln, a_re=a_re, a_im=a_im):
            h_re, h_im = h
            rows = pl.ds(t * rps, rps)
            n_re = a_re * h_re - a_im * h_im + sre[rows, ln]
            n_im = a_re * h_im + a_im * h_re + sim[rows, ln]
            sre[rows, ln] = n_re
            sim[rows, ln] = n_im
            return n_re, n_im

        if steps <= 8:
            h = h0
            for t in range(steps):
                h = step(t, h)
        else:
            h = lax.fori_loop(0, steps, step, h0, unroll=8)
        if carry:
            cre_s[0:rps, ln] = h[0]
            cim_s[0:rps, ln] = h[1]
        else:
            hr_out[:, ln] = h[0]
            hi_out[:, ln] = h[1]
    if carry:
        hr_out[0] = cre_s[...]
        hi_out[0] = cim_s[...]

    for q in range(S5_QBLOCKS):
        st = slice(q * S5_QST, (q + 1) * S5_QST)
        yq = _dot(sre[:, st].astype(BF16), cre_ref[q]) - _dot(sim[:, st].astype(BF16), cim_ref[q])
        ysc[:, q * S5_QCH:(q + 1) * S5_QCH] = yq
    y = ysc[...]
    if perm:
        y_hi, y_lo = _split2(y)
        y = _dot(pt_ref[...], y_hi) + _dot(pt_ref[...], y_lo)
    y_ref[...] = y + d_ref[...] * u


def _perm_matrix(steps, rps):
    n = steps * rps
    j = np.arange(n)
    i = (j % rps) * steps + j // rps
    p = np.zeros((n, n), np.float32)
    p[j, i] = 1.0
    return p


def _s5_branch(proj, row0, nseq, seqlen, wts, h0=None):
    bre, bim, cre, cim, a_re, a_im, d = wts
    nrows = nseq * seqlen
    assert row0 % ROWS == 0 and nrows % ROWS == 0
    blk0 = row0 // ROWS
    carry = h0 is None
    if carry:
        assert seqlen % ROWS == 0
        nchunk = seqlen // ROWS
        grid = (nseq, nchunk)
        steps, rps, perm = ROWS, 1, False
        row_map = lambda b, c: (blk0 + b * nchunk + c, 0)
        out_map = lambda b, c: (b * nchunk + c, 0)
        const2 = lambda b, c: (0, 0)
        const3 = lambda b, c: (0, 0, 0)
        st_shape = jax.ShapeDtypeStruct((nseq, SUBLANES, S5_STATE), F32)
        st_spec = pl.BlockSpec((1, SUBLANES, S5_STATE), lambda b, c: (b, 0, 0))
        sem = ("arbitrary", "arbitrary")
    else:
        assert ROWS % seqlen == 0
        steps, rps, perm = seqlen, ROWS // seqlen, True
        grid = (nrows // ROWS,)
        row_map = lambda i: (blk0 + i, 0)
        out_map = lambda i: (i, 0)
        const2 = lambda i: (0, 0)
        const3 = lambda i: (0, 0, 0)
        st_shape = jax.ShapeDtypeStruct((nseq, S5_STATE), F32)
        st_spec = pl.BlockSpec((rps, S5_STATE), lambda i: (i, 0))
        sem = ("parallel",)
    in_specs = [
        pl.BlockSpec((ROWS, W_A), row_map),
        pl.BlockSpec(bre.shape, const3), pl.BlockSpec(bim.shape, const3),
        pl.BlockSpec(cre.shape, const3), pl.BlockSpec(cim.shape, const3),
        pl.BlockSpec((1, S5_STATE), const2), pl.BlockSpec((1, S5_STATE), const2),
        pl.BlockSpec((1, W_A), const2),
    ]
    args = [proj, bre, bim, cre, cim, a_re, a_im, d]
    if perm:
        pm = _perm_matrix(steps, rps)
        in_specs += [pl.BlockSpec((ROWS, ROWS), const2), pl.BlockSpec((ROWS, ROWS), const2)]
        args += [jnp.asarray(pm, BF16), jnp.asarray(pm.T, BF16)]
    if not carry:
        in_specs += [st_spec, st_spec]
        args += [h0[0], h0[1]]
    scratch = [pltpu.VMEM((ROWS, S5_STATE), F32), pltpu.VMEM((ROWS, S5_STATE), F32), pltpu.VMEM((ROWS, W_A), F32)]
    if carry:
        scratch += [pltpu.VMEM((SUBLANES, S5_STATE), F32), pltpu.VMEM((SUBLANES, S5_STATE), F32)]
    y, hr, hi = pl.pallas_call(
        functools.partial(_s5_kernel, steps=steps, rps=rps, perm=perm, carry=carry),
        grid=grid,
        in_specs=in_specs,
        out_specs=[pl.BlockSpec((ROWS, W_A), out_map), st_spec, st_spec],
        out_shape=[jax.ShapeDtypeStruct((nrows, W_A), F32), st_shape, st_shape],
        scratch_shapes=scratch,
        compiler_params=_params(sem),
        name="s5_prompt" if carry else "s5_decode",
    )(*args)
    if carry:
        hr, hi = hr[:, 0], hi[:, 0]
    return y, hr, hi


def _softplus(x):
    return jnp.maximum(x, 0.0) + jnp.log1p(jnp.exp(-jnp.abs(x)))


def _ssd_kernel(*refs, ls, nseq, nreal, carry):
    it = iter(refs)
    xbc_ref, dtr_ref = next(it), next(it)
    cc_refs = None if carry else (next(it), next(it), next(it))
    convw_ref, convb_ref, dtb_ref, alog_ref, dx_ref, e_ref = (next(it) for _ in range(6))
    st_in = None if carry else next(it)
    y_ref, st_out = next(it), next(it)
    xp, act, ysc, dts = next(it), next(it), next(it), next(it)
    st_sc = tail = None
    if carry:
        st_sc, tail = next(it), next(it)
        st_in = st_sc
    r = nreal
    shift = int(math.log2(ls))
    assert 1 << shift == ls

    xp[SUBLANES:SUBLANES + r, :] = xbc_ref[...]
    if carry:
        @pl.when(pl.program_id(1) == 0)
        def _():
            tail[...] = jnp.zeros_like(tail)
            st_sc[...] = jnp.zeros_like(st_sc)
        xp[0:SUBLANES, :] = tail[...]
        tail[...] = xp[r:r + SUBLANES, :]
    else:
        xp[0:SUBLANES, :] = jnp.zeros((SUBLANES, CONV_DIM), F32)
        if r < ROWS:
            act[r:ROWS, :] = jnp.zeros((ROWS - r, CONV_DIM), F32)
        pos = lax.broadcasted_iota(jnp.int32, (r, 1), 0) & (ls - 1)
    for cbk in range(CONV_DIM // SCAN_LANES):
        ln = slice(cbk * SCAN_LANES, (cbk + 1) * SCAN_LANES)
        acc = convb_ref[:, ln] + convw_ref[CONV_K - 1:CONV_K, ln] * xp[SUBLANES:SUBLANES + r, ln]
        for d in range(1, CONV_K):
            sh = xp[SUBLANES - d:SUBLANES - d + r, ln]
            if not carry:
                sh = jnp.where(pos >= d, sh, cc_refs[d - 1][:, ln])
            acc = acc + convw_ref[CONV_K - 1 - d:CONV_K - d, ln] * sh
        act[0:r, ln] = acc * jax.nn.sigmoid(acc)

    dt = _softplus(dtr_ref[...] + dtb_ref[...])
    if r < ROWS:
        dts[...] = jnp.zeros_like(dts)
        dts[0:r, :] = dt
        dt = dts[...]
    da = dt * (-jnp.exp(alog_ref[...]))
    ri = lax.broadcasted_iota(jnp.int32, (ROWS, ROWS), 0)
    ci = lax.broadcasted_iota(jnp.int32, (ROWS, ROWS), 1)
    same = lax.shift_right_logical(ri, shift) == lax.shift_right_logical(ci, shift)
    mask = same & (ci <= ri)
    da3 = _split3(da)
    mb = mask.astype(F32).astype(BF16)
    sb = same.astype(F32).astype(BF16)
    cs = _dot(mb, da3[0]) + _dot(mb, da3[1]) + _dot(mb, da3[2])
    cse = _dot(sb, da3[0]) + _dot(sb, da3[1]) + _dot(sb, da3[2])
    cst = cs.T

    e = e_ref[...]

    def expand(v):
        hi, lo = _split2(v)
        return _dot(hi, e) + _dot(lo, e)

    xs = act[:, 0:W_S]
    xd = xs * expand(dt)
    xdw = xd * expand(jnp.exp(cse - cs))
    ecs_x = expand(jnp.exp(cs))
    dect = expand(jnp.exp(cse)).T
    rowseq = lax.shift_right_logical(lax.broadcasted_iota(jnp.int32, (ROWS, 1), 0), shift)
    gw = W_S // SSD_GROUPS
    lane_head = lax.broadcasted_iota(jnp.int32, (ROWS, gw), 1) // SSD_HEADDIM

    for g in range(SSD_GROUPS):
        gl = slice(g * gw, (g + 1) * gw)
        bg_f = act[:, W_S + g * SSD_N:W_S + (g + 1) * SSD_N]
        cg = act[:, W_S + SSD_GROUPS * SSD_N + g * SSD_N:W_S + SSD_GROUPS * SSD_N + (g + 1) * SSD_N].astype(BF16)
        bg = bg_f.astype(BF16)
        cb = _dot_nt(cg, bg)
        xdg = xd[:, gl].astype(BF16)
        yg = jnp.zeros((ROWS, gw), F32)
        for hh in range(gw // SSD_HEADDIM):
            h = g * (gw // SSD_HEADDIM) + hh
            diff = cs[:, h:h + 1] - cst[h:h + 1, :]
            lmat = jnp.exp(jnp.where(mask, diff, -jnp.inf))
            yh = _dot((cb * lmat).astype(BF16), xdg)
            yg = jnp.where(lane_head == hh, yh, yg)
        xdwt = xdw[:, gl].T.astype(BF16)
        yoff = jnp.zeros((ROWS, gw), F32)
        for s in range(nseq):
            st = st_in[s, gl, :]
            z = _dot_nt(cg, st.astype(BF16))
            if nseq > 1:
                rowm = rowseq == s
                z = jnp.where(rowm, z, 0.0)
                bgs = jnp.where(rowm, bg_f, 0.0).astype(BF16)
            else:
                bgs = bg
            yoff = yoff + z
            dcol = dect[gl, s * ls:s * ls + 1]
            st_out_val = dcol * st + _dot(xdwt, bgs)
            if carry:
                st_sc[s, gl, :] = st_out_val
            else:
                st_out[s, gl, :] = st_out_val
        ysc[:, gl] = yg + yoff * ecs_x[:, gl]
    y = ysc[...] + dx_ref[...] * xs
    y_ref[...] = y[0:r]
    if carry:
        @pl.when(pl.program_id(1) == pl.num_programs(1) - 1)
        def _():
            st_out[...] = st_sc[...]


def _ssd_branch(proj, dtraw, row0, nseq, seqlen, wts, state=None, cache=None):
    convw, convb, dtb, alog, dx, e = wts
    nrows = nseq * seqlen
    carry = state is None
    xcol = 2 * W_A // CONV_DIM
    assert xcol * CONV_DIM == 2 * W_A
    const2 = lambda *_: (0, 0)
    if carry:
        assert seqlen % ROWS == 0 and row0 % ROWS == 0
        nchunk = seqlen // ROWS
        blk0 = row0 // ROWS
        grid = (nseq, nchunk)
        nreal, ls, ns = ROWS, ROWS, 1
        row_map = lambda b, c: (blk0 + b * nchunk + c, xcol)
        dt_map = lambda b, c: (blk0 + b * nchunk + c, 0)
        out_map = lambda b, c: (b * nchunk + c, 0)
        st_spec = pl.BlockSpec((1, W_S, SSD_N), lambda b, c: (b, 0, 0))
        sem = ("arbitrary", "arbitrary")
        extra_in, extra_specs = [], []
        st_args, st_specs = [], []
    else:
        ns = 8
        ls = seqlen
        nreal = ns * seqlen
        assert nreal % SUBLANES == 0 and nreal <= ROWS and row0 % nreal == 0 and nseq % ns == 0
        blk0 = row0 // nreal
        grid = (nseq // ns,)
        row_map = lambda i: (blk0 + i, xcol)
        dt_map = lambda i: (blk0 + i, 0)
        out_map = lambda i: (i, 0)
        st_spec = pl.BlockSpec((ns, W_S, SSD_N), lambda i: (i, 0, 0))
        sem = ("parallel",)
        extra_in = [jnp.pad(cache[:, CONV_K - 1 - d:, :], ((0, 0), (0, seqlen - d), (0, 0))).reshape(nrows, CONV_DIM)
                    for d in range(1, CONV_K)]
        extra_specs = [pl.BlockSpec((nreal, CONV_DIM), lambda i: (i, 0))] * (CONV_K - 1)
        st_args, st_specs = [state], [st_spec]
    in_specs = ([pl.BlockSpec((nreal, CONV_DIM), row_map), pl.BlockSpec((nreal, LANES), dt_map)] + extra_specs +
                [pl.BlockSpec((SUBLANES, CONV_DIM), const2), pl.BlockSpec((1, CONV_DIM), const2),
                 pl.BlockSpec((1, LANES), const2), pl.BlockSpec((1, LANES), const2),
                 pl.BlockSpec((1, W_S), const2), pl.BlockSpec((LANES, W_S), const2)] + st_specs)
    scratch = [pltpu.VMEM((ROWS + SUBLANES, CONV_DIM), F32), pltpu.VMEM((ROWS, CONV_DIM), F32),
               pltpu.VMEM((ROWS, W_S), F32), pltpu.VMEM((ROWS, LANES), F32)]
    if carry:
        scratch += [pltpu.VMEM((1, W_S, SSD_N), F32), pltpu.VMEM((SUBLANES, CONV_DIM), F32)]
    y, st = pl.pallas_call(
        functools.partial(_ssd_kernel, ls=ls, nseq=ns, nreal=nreal, carry=carry),
        grid=grid,
        in_specs=in_specs,
        out_specs=[pl.BlockSpec((nreal, W_S), out_map), st_spec],
        out_shape=[jax.ShapeDtypeStruct((nrows, W_S), F32), jax.ShapeDtypeStruct((nseq, W_S, SSD_N), F32)],
        scratch_shapes=scratch,
        compiler_params=_params(sem),
        name="ssd_prompt" if carry else "ssd_decode",
    )(proj, dtraw, *extra_in, convw, convb, dtb, alog, dx, e, *st_args)
    return y, st


def _mix_kernel(ya_ref, za_ref, ys_ref, zs_ref, gw_ref, gb_ref, na_ref, ns_ref, o_ref):
    def rms(v, w):
        return v * lax.rsqrt(jnp.mean(v * v, axis=-1, keepdims=True) + EPS) * w

    g = jax.nn.gelu(ya_ref[...], approximate=True)
    gate = jax.nn.sigmoid(_dot(g.astype(BF16), gw_ref[...]) + gb_ref[...])
    za = za_ref[...]
    ya = g * gate * (za * jax.nn.sigmoid(za))
    o_ref[:, 0:W_A] = rms(ya, na_ref[...]).astype(o_ref.dtype)
    zs = zs_ref[...]
    ys = ys_ref[...] * (zs * jax.nn.sigmoid(zs))
    o_ref[:, W_A:W_A + W_S] = rms(ys, ns_ref[...]).astype(o_ref.dtype)


def _mix(ya, ys, proj, glu_w, glu_b, na_w, ns_w):
    m = ya.shape[0]
    tm = _tile(m, 256)
    row = lambda i: (i, 0)
    const = lambda i: (0, 0)
    zs_col = (2 * W_A + CONV_DIM) // W_S
    return pl.pallas_call(
        _mix_kernel,
        grid=(m // tm,),
        in_specs=[pl.BlockSpec((tm, W_A), row), pl.BlockSpec((tm, W_A), lambda i: (i, 1)),
                  pl.BlockSpec((tm, W_S), row), pl.BlockSpec((tm, W_S), lambda i: (i, zs_col)),
                  pl.BlockSpec((W_A, W_A), const), pl.BlockSpec((1, W_A), const),
                  pl.BlockSpec((1, W_A), const), pl.BlockSpec((1, W_S), const)],
        out_specs=pl.BlockSpec((tm, W_A + W_S), row),
        out_shape=jax.ShapeDtypeStruct((m, W_A + W_S), BF16),
        compiler_params=_params(("parallel",)),
        name="glu_mix",
    )(ya, proj, ys, proj, glu_w, glu_b.reshape(1, W_A), na_w.reshape(1, W_A), ns_w.reshape(1, W_S))


def _pad_lanes(v):
    return jnp.pad(v.reshape(1, -1), ((0, 0), (0, LANES - v.shape[-1])))


def kernel(x_prompt, x_sample, state_s5_re, state_s5_im, state_ssd, cache_conv, norm_w, w_in, s5_lambda_re, s5_lambda_im, s5_log_step, s5_b_re, s5_b_im, s5_c_re, s5_c_im, s5_d, s5_glu_w, s5_glu_b, s5_norm_w, conv_w, conv_b, dt_bias, a_log, ssd_d, ssd_norm_w, w_out, final_norm_w):
    bp, lp, _ = x_prompt.shape
    bs, ls, _ = x_sample.shape
    depth = norm_w.shape[0]
    mp, ms = bp * lp, bs * ls
    x = jnp.concatenate([x_prompt.reshape(mp, D_MODEL), x_sample.reshape(ms, D_MODEL)], axis=0)
    head_expand = jnp.asarray(np.concatenate([np.kron(np.eye(SSD_HEADS, dtype=np.float32),
                                                      np.ones((1, SSD_HEADDIM), np.float32)),
                                              np.zeros((LANES - SSD_HEADS, W_S), np.float32)], axis=0), BF16)
    outs = {k: [] for k in ("p_re", "p_im", "p_ssd", "p_conv", "s_re", "s_im", "s_ssd", "s_conv")}
    y_final = None
    for l in range(depth):
        xn = _rmsnorm_bf16(x, norm_w[l])
        w_l = w_in[l].astype(BF16)
        proj = _matmul(xn, w_l[:, :MAIN_COLS], tm=512, tn=2048, name="in_proj")
        w_dt = jnp.pad(w_l[:, MAIN_COLS:], ((0, 0), (0, LANES - SSD_HEADS)))
        dtraw = _matmul(xn, w_dt, tm=512, tn=LANES, name="in_proj_dt")

        bb_re, bb_im, ab_re, ab_im = _s5_prep(s5_lambda_re[l], s5_lambda_im[l], s5_log_step[l], s5_b_re[l], s5_b_im[l])
        s5_w = (_blockdiag_in(bb_re), _blockdiag_in(bb_im), _blockdiag_out(s5_c_re[l]), _blockdiag_out(s5_c_im[l]),
                ab_re.reshape(1, S5_STATE), ab_im.reshape(1, S5_STATE), s5_d[l].reshape(1, W_A))
        ya_p, pre, pim = _s5_branch(proj, 0, bp, lp, s5_w)
        ya_s, sre, sim = _s5_branch(proj, mp, bs, ls, s5_w,
                                    h0=(state_s5_re[l].reshape(bs, S5_STATE), state_s5_im[l].reshape(bs, S5_STATE)))

        ssd_w = (jnp.pad(conv_w[l], ((0, SUBLANES - CONV_K), (0, 0))), conv_b[l].reshape(1, CONV_DIM),
                 _pad_lanes(dt_bias[l]), _pad_lanes(a_log[l]),
                 jnp.repeat(ssd_d[l], SSD_HEADDIM).reshape(1, W_S), head_expand)
        ys_p, pssd = _ssd_branch(proj, dtraw, 0, bp, lp, ssd_w)
        ys_s, sssd = _ssd_branch(proj, dtraw, mp, bs, ls, ssd_w,
                                 state=state_ssd[l].reshape(bs, W_S, SSD_N), cache=cache_conv[l])

        ymix = _mix(jnp.concatenate([ya_p, ya_s], axis=0), jnp.concatenate([ys_p, ys_s], axis=0), proj,
                    s5_glu_w[l].astype(BF16), s5_glu_b[l], s5_norm_w[l], ssd_norm_w[l])
        last = l == depth - 1
        x_new = _matmul(ymix, w_out[l].astype(BF16), tm=256, tn=D_MODEL, res=x,
                        norm_w=final_norm_w if last else None, name="out_proj")
        if last:
            y_final = x_new
        else:
            x = x_new

        xbc = proj[:, 2 * W_A:2 * W_A + CONV_DIM]
        outs["p_re"].append(pre.reshape(bp, S5_GROUPS, S5_P))
        outs["p_im"].append(pim.reshape(bp, S5_GROUPS, S5_P))
        outs["p_ssd"].append(pssd.reshape(bp, SSD_HEADS, SSD_HEADDIM, SSD_N))
        outs["p_conv"].append(xbc[:mp].reshape(bp, lp, CONV_DIM)[:, lp - (CONV_K - 1):, :])
        outs["s_re"].append(sre.reshape(bs, S5_GROUPS, S5_P))
        outs["s_im"].append(sim.reshape(bs, S5_GROUPS, S5_P))
        outs["s_ssd"].append(sssd.reshape(bs, SSD_HEADS, SSD_HEADDIM, SSD_N))
        xp_s = jnp.concatenate([cache_conv[l], xbc[mp:].reshape(bs, ls, CONV_DIM)], axis=1)
        outs["s_conv"].append(xp_s[:, ls:, :])
    return (y_final[:mp].reshape(bp, lp, D_MODEL), y_final[mp:].reshape(bs, ls, D_MODEL),
            jnp.stack(outs["p_re"]), jnp.stack(outs["p_im"]), jnp.stack(outs["p_ssd"]), jnp.stack(outs["p_conv"]),
            jnp.stack(outs["s_re"]), jnp.stack(outs["s_im"]), jnp.stack(outs["s_ssd"]), jnp.stack(outs["s_conv"]))
```
